```python
import math
import jax
import jax.numpy as jnp
from jax import lax
import numpy as np

D_MODEL = 1024
BATCH = 4
SEQ = 4096
DEPTH = 4
DEC_BATCH = 128
DEC_SEQ = 8
PAST_LEN = 2048
PAGE_SIZE = 128

N_META = 16
N_EVEN = (DEPTH + 1) // 2
N_ODD = DEPTH // 2
A_HEAD_DIM = 64
A_HEADS = (D_MODEL // 2) // A_HEAD_DIM
A_KV_HEADS = 2
IDX_HEADS = 4
IDX_DIM = 64
TOPK_MAX = 256
META_BONUS = 1e30
Q_BLOCK = 128
ROPE_THETA = 500000.0
ROPE_FRAC = 4
DN_DK = 128
DN_DV = 128
DN_HEADS = (D_MODEL // 2) // DN_DV
CONV_W = 4
HG_DK = 128
HG_HEADS = D_MODEL // HG_DK
HG_DV = D_MODEL // HG_HEADS
CHUNK = 64
D_FF = 4 * D_MODEL

A_Q_W = A_HEADS * A_HEAD_DIM
A_KV_W = A_KV_HEADS * A_HEAD_DIM
IDX_Q_W = IDX_HEADS * IDX_DIM
DN_QK_W = DN_HEADS * DN_DK
DN_V_W = DN_HEADS * DN_DV
CONV_DIM = 2 * DN_QK_W + DN_V_W
EVEN_SPLITS = (A_Q_W, A_KV_W, A_KV_W, IDX_Q_W, IDX_DIM, IDX_HEADS, CONV_DIM, DN_HEADS, DN_HEADS, DN_V_W)
EVEN_IN = A_Q_W + 2 * A_KV_W + IDX_Q_W + IDX_DIM + IDX_HEADS + CONV_DIM + 2 * DN_HEADS + DN_V_W
EVEN_OUT = A_Q_W + DN_V_W
HG_F = HG_HEADS * HG_DK
HG_V = HG_HEADS * HG_DV
ODD_IN = 2 * HG_F + 2 * HG_V

kernel_name = 'hybrid_dsa_gdn_hgrn2_step'


def _rmsnorm(x, g, eps=1e-6):
    xf = x.astype(jnp.float32)
    y = xf * lax.rsqrt(jnp.mean(xf * xf, axis=-1, keepdims=True) + eps)
    return (y * g.astype(jnp.float32)).astype(x.dtype)


def _l2norm(x, eps=1e-6):
    xf = x.astype(jnp.float32)
    return xf * lax.rsqrt(jnp.sum(xf * xf, axis=-1, keepdims=True) + eps)


def _partial_rope(x, pos):
    rot = x.shape[-1] // ROPE_FRAC
    half = rot // 2
    inv = ROPE_THETA ** (-jnp.arange(half, dtype=jnp.float32) / half)
    ang = pos.astype(jnp.float32)[:, None] * inv[None, :]
    shape = (1, pos.shape[0]) + (1,) * (x.ndim - 3) + (half,)
    cos = jnp.cos(ang).reshape(shape)
    sin = jnp.sin(ang).reshape(shape)
    xf = x.astype(jnp.float32)
    x1 = xf[..., :half]
    x2 = xf[..., half:rot]
    out = jnp.concatenate([x1 * cos - x2 * sin, x2 * cos + x1 * sin, xf[..., rot:]], axis=-1)
    return out.astype(x.dtype)


def _gather_pages(pool, page_table):
    g = pool[page_table]
    return g.reshape((g.shape[0], g.shape[1] * g.shape[2]) + g.shape[3:])


def _short_conv(u, buf, w):
    full = jnp.concatenate([buf.astype(u.dtype), u], axis=1)
    out = lax.conv_general_dilated(full, w[:, None, :].astype(u.dtype), window_strides=(1,), padding='VALID',
                                   dimension_numbers=('NWC', 'WIO', 'NWC'), feature_group_count=u.shape[-1])
    return jax.nn.silu(out), full[:, full.shape[1] - (CONV_W - 1):, :]


def _dsa_attention(q, iq, iw, k, v, ik, q_pos, k_sel):
    b, L, h, hd = q.shape
    s = k.shape[1]
    qb = min(Q_BLOCK, L)
    nb = -(-L // qb)
    pad = nb * qb - L

    def blocks(t):
        t = jnp.pad(t, [(0, 0), (0, pad)] + [(0, 0)] * (t.ndim - 2))
        return jnp.moveaxis(t.reshape((b, nb, qb) + t.shape[2:]), 1, 0)

    pos_b = jnp.pad(q_pos, (0, pad), mode='edge').reshape(nb, qb)
    key_pos = jnp.arange(s, dtype=jnp.int32)
    is_meta = key_pos < N_META
    ikf = ik.astype(jnp.float32)
    rep = h // A_KV_HEADS
    idx_scale = (IDX_HEADS * IDX_DIM) ** -0.5
    att_scale = hd ** -0.5

    def one_block(args):
        qq, qi, wi, pp = args
        rel = jax.nn.relu(jnp.einsum('bqhd,bsd->bqhs', qi.astype(jnp.float32), ikf))
        score = jnp.einsum('bqh,bqhs->bqs', wi.astype(jnp.float32) * idx_scale, rel)
        score = jnp.where(is_meta, META_BONUS, score)
        visible = key_pos[None, :] <= pp[:, None]
        score = jnp.where(visible[None], score, -jnp.inf)
        _, sel = lax.top_k(score, k_sel)
        valid = sel <= pp[None, :, None]
        kg = jax.vmap(lambda kk, ii: kk[ii])(k, sel).astype(jnp.float32)
        vg = jax.vmap(lambda vv, ii: vv[ii])(v, sel).astype(jnp.float32)
        qg = qq.astype(jnp.float32).reshape(b, qb, A_KV_HEADS, rep, hd)
        logits = jnp.einsum('bqgrd,bqkgd->bqgrk', qg, kg) * att_scale
        logits = jnp.where(valid[:, :, None, None, :], logits, -jnp.inf)
        p = jax.nn.softmax(logits, axis=-1)
        o = jnp.einsum('bqgrk,bqkgd->bqgrd', p, vg)
        return o.reshape(b, qb, h * hd).astype(q.dtype)

    out = lax.map(one_block, (blocks(q), blocks(iq), blocks(iw), pos_b))
    return jnp.moveaxis(out, 0, 1).reshape(b, nb * qb, h * hd)[:, :L]


def _to_chunks(t, n, c):
    t = t.reshape((t.shape[0], n, c) + t.shape[2:])
    return jnp.transpose(t, (1, 0, 3, 2) + tuple(range(4, t.ndim)))


def _from_chunks(o):
    n, b, h, c, d = o.shape
    return jnp.transpose(o, (1, 0, 3, 2, 4)).reshape(b, n * c, h, d)


def _gated_delta_chunks(q, k, v, g, beta, s0, chunk):
    b, L, h, dk = q.shape
    dv = v.shape[-1]
    n = L // chunk
    qc, kc, vc, gc, bc = (_to_chunks(t.astype(jnp.float32), n, chunk) for t in (q, k, v, g, beta))
    incl = jnp.tril(jnp.ones((chunk, chunk), bool))
    strict = jnp.tril(jnp.ones((chunk, chunk), bool), k=-1)
    eye = jnp.eye(chunk, dtype=jnp.float32)

    def step(S, inp):
        qq, kk, vv, gg, bb = inp
        gam = jnp.cumsum(gg, axis=-1)
        diff = gam[..., :, None] - gam[..., None, :]
        dec = jnp.exp(jnp.where(incl, diff, -jnp.inf))
        kkt = jnp.einsum('bhtd,bhsd->bhts', kk, kk)
        m = jnp.where(strict, bb[..., :, None] * kkt * dec, 0.0) + eye
        rhs = jnp.concatenate([vv * bb[..., None], kk * (bb * jnp.exp(gam))[..., None]], axis=-1)
        sol = lax.linalg.triangular_solve(m, rhs, left_side=True, lower=True, unit_diagonal=True)
        u, w = sol[..., :dv], sol[..., dv:]
        v_new = u - jnp.einsum('bhtd,bhde->bhte', w, S)
        attn = jnp.einsum('bhtd,bhsd->bhts', qq, kk) * dec
        o = jnp.einsum('bhtd,bhde->bhte', qq * jnp.exp(gam)[..., None], S) + jnp.einsum('bhts,bhse->bhte', attn, v_new)
        gl = gam[..., -1:]
        S = S * jnp.exp(gl)[..., None] + jnp.einsum('bhsd,bhse->bhde', kk * jnp.exp(gl - gam)[..., None], v_new)
        return S, o

    S, o = lax.scan(step, s0.astype(jnp.float32), (qc, kc, vc, gc, bc))
    return _from_chunks(o), S


def _hgrn2_chunks(q, k, v, logf, s0, chunk):
    b, L, h, dk = q.shape
    n = L // chunk
    qc, kc, vc, fc = (_to_chunks(t.astype(jnp.float32), n, chunk) for t in (q, k, v, logf))
    incl = jnp.tril(jnp.ones((chunk, chunk), bool))[:, :, None]

    def step(S, inp):
        qq, kk, vv, ff = inp
        cb = jnp.cumsum(ff, axis=-2)
        diff = cb[..., :, None, :] - cb[..., None, :, :]
        dec = jnp.exp(jnp.where(incl, diff, -jnp.inf))
        attn = jnp.sum(qq[..., :, None, :] * dec * kk[..., None, :, :], axis=-1)
        o = jnp.einsum('bhtd,bhde->bhte', qq * jnp.exp(cb), S) + jnp.einsum('bhts,bhse->bhte', attn, vv)
        cl = cb[..., -1:, :]
        S = S * jnp.exp(cl[..., 0, :])[..., None] + jnp.einsum('bhsd,bhse->bhde', kk * jnp.exp(cl - cb), vv)
        return S, o

    S, o = lax.scan(step, s0.astype(jnp.float32), (qc, kc, vc, fc))
    return _from_chunks(o), S


def _chunked(fn, seqs, s0, pad_front, chunk):
    L = seqs[0].shape[1]
    pad_end = (-(pad_front + L)) % chunk
    padded = [jnp.pad(t, [(0, 0), (pad_front, pad_end)] + [(0, 0)] * (t.ndim - 2)) for t in seqs]
    o, S = fn(*padded, s0, chunk)
    return o[:, pad_front:pad_front + L], S


def _even_mixer(hn, pos, past_k, past_v, past_ik, conv_buf, s0, k_sel, pad_front, chunk,
                w_in, w_out, conv_w, a_log, dt_bias, dn_norm):
    b, L, _ = hn.shape
    offs = np.cumsum(EVEN_SPLITS)[:-1].tolist()
    aq, ak, av, iq, ik, iw, dqkv, db, da, dz = jnp.split(hn @ w_in, offs, axis=-1)
    aq = _partial_rope(aq.reshape(b, L, A_HEADS, A_HEAD_DIM), pos)
    ak = _partial_rope(ak.reshape(b, L, A_KV_HEADS, A_HEAD_DIM), pos)
    av = av.reshape(b, L, A_KV_HEADS, A_HEAD_DIM)
    iq = _partial_rope(iq.reshape(b, L, IDX_HEADS, IDX_DIM), pos)
    ik = _partial_rope(ik, pos)
    keys = jnp.concatenate([past_k.astype(ak.dtype), ak], axis=1)
    vals = jnp.concatenate([past_v.astype(av.dtype), av], axis=1)
    ikeys = jnp.concatenate([past_ik.astype(ik.dtype), ik], axis=1)
    a_out = _dsa_attention(aq, iq, iw, keys, vals, ikeys, pos, k_sel)
    c, new_buf = _short_conv(dqkv, conv_buf, conv_w)
    dq, dk, dv = jnp.split(c, [DN_QK_W, 2 * DN_QK_W], axis=-1)
    dq = _l2norm(dq.reshape(b, L, DN_HEADS, DN_DK)) * (DN_DK ** -0.5)
    dk = _l2norm(dk.reshape(b, L, DN_HEADS, DN_DK))
    dv = dv.reshape(b, L, DN_HEADS, DN_DV)
    beta = jax.nn.sigmoid(db.astype(jnp.float32))
    g = -jnp.exp(a_log.astype(jnp.float32)) * jax.nn.softplus(da.astype(jnp.float32) + dt_bias.astype(jnp.float32))
    o, s_new = _chunked(_gated_delta_chunks, (dq, dk, dv, g, beta), s0, pad_front, chunk)
    o = _rmsnorm(o.astype(hn.dtype), dn_norm) * jax.nn.silu(dz.reshape(b, L, DN_HEADS, DN_DV))
    mixed = jnp.concatenate([a_out, o.reshape(b, L, DN_V_W)], axis=-1)
    return mixed @ w_out, ak, av, ik, new_buf, s_new


def _odd_mixer(hn, s0, lb, pad_front, chunk, w_in, w_out, hg_norm):
    b, L, _ = hn.shape
    q, f, iv, gz = jnp.split(hn @ w_in, [HG_F, 2 * HG_F, 2 * HG_F + HG_V], axis=-1)
    q = jax.nn.silu(q).reshape(b, L, HG_HEADS, HG_DK)
    ff = f.astype(jnp.float32).reshape(b, L, HG_HEADS, HG_DK)
    lbh = lb.reshape(HG_HEADS, HG_DK)
    logf = jnp.logaddexp(jnp.log(lbh), jnp.log1p(-lbh) + jax.nn.log_sigmoid(ff))
    kk = (1.0 - lbh) * jax.nn.sigmoid(-ff)
    v = iv.reshape(b, L, HG_HEADS, HG_DV)
    o, s_new = _chunked(_hgrn2_chunks, (q, kk, v, logf), s0, pad_front, chunk)
    o = _rmsnorm(o.astype(hn.dtype), hg_norm) * jax.nn.silu(gz.reshape(b, L, HG_HEADS, HG_DV))
    return o.reshape(b, L, HG_V) @ w_out, s_new


def _sq_relu_mlp(x, w_up, w_down):
    return jnp.square(jax.nn.relu(x @ w_up)) @ w_down


def setup_inputs(seed: int = 0) -> dict:
    key = jax.random.key(seed)
    ks = iter(jax.random.split(key, 32))

    def nrm(shape, scale):
        return scale * jax.random.normal(next(ks), shape, jnp.float32)

    n_pages = PAST_LEN // PAGE_SIZE
    n_used = DEC_BATCH * n_pages
    n_pool = n_used + n_used // 4
    page_table = jax.random.permutation(next(ks), n_pool)[:n_used].reshape(DEC_BATCH, n_pages).astype(jnp.int32)
    dn_a_log = jnp.log(jax.random.uniform(next(ks), (N_EVEN, DN_HEADS), jnp.float32, 1.0, 16.0))
    dt = jnp.exp(jax.random.uniform(next(ks), (N_EVEN, DN_HEADS), jnp.float32, math.log(1e-3), math.log(0.1)))
    dn_dt_bias = dt + jnp.log(-jnp.expm1(-dt))
    return {
        'x_prompt': nrm((BATCH, SEQ, D_MODEL), 1.0),
        'x_sample': nrm((DEC_BATCH, DEC_SEQ, D_MODEL), 1.0),
        'cache_k': nrm((N_EVEN, n_pool, PAGE_SIZE, A_KV_HEADS, A_HEAD_DIM), 1.0),
        'cache_v': nrm((N_EVEN, n_pool, PAGE_SIZE, A_KV_HEADS, A_HEAD_DIM), 1.0),
        'cache_ik': nrm((N_EVEN, n_pool, PAGE_SIZE, IDX_DIM), 1.0),
        'state_conv': nrm((N_EVEN, DEC_BATCH, CONV_W - 1, CONV_DIM), 1.0),
        'state_delta': nrm((N_EVEN, DEC_BATCH, DN_HEADS, DN_DK, DN_DV), 0.3),
        'state_hgrn': nrm((N_ODD, DEC_BATCH, HG_HEADS, HG_DK, HG_DV), 0.5),
        'page_table': page_table,
        'meta': nrm((N_META, D_MODEL), 1.0),
        'ln_mix': 1.0 + nrm((DEPTH, D_MODEL), 0.1),
        'ln_ffn': 1.0 + nrm((DEPTH, D_MODEL), 0.1),
        'ln_final': 1.0 + nrm((D_MODEL,), 0.1),
        'w_in_e': nrm((N_EVEN, D_MODEL, EVEN_IN), D_MODEL ** -0.5),
        'w_out_e': nrm((N_EVEN, EVEN_OUT, D_MODEL), EVEN_OUT ** -0.5),
        'dn_conv': nrm((N_EVEN, CONV_W, CONV_DIM), CONV_W ** -0.5),
        'dn_a_log': dn_a_log,
        'dn_dt_bias': dn_dt_bias,
        'dn_norm': 1.0 + nrm((N_EVEN, DN_DV), 0.1),
        'w_in_o': nrm((N_ODD, D_MODEL, ODD_IN), D_MODEL ** -0.5),
        'w_out_o': nrm((N_ODD, HG_V, D_MODEL), HG_V ** -0.5),
        'hg_lb': nrm((DEPTH, HG_F), 0.5),
        'hg_norm': 1.0 + nrm((N_ODD, HG_DV), 0.1),
        'w_up': nrm((DEPTH, D_MODEL, D_FF), D_MODEL ** -0.5),
        'w_down': nrm((DEPTH, D_FF, D_MODEL), D_FF ** -0.5),
    }


def reference(x_prompt, x_sample, cache_k, cache_v, cache_ik, state_conv, state_delta, state_hgrn, page_table,
              meta, ln_mix, ln_ffn, ln_final, w_in_e, w_out_e, dn_conv, dn_a_log, dn_dt_bias, dn_norm,
              w_in_o, w_out_o, hg_lb, hg_norm, w_up, w_down):
    dt = x_prompt.dtype
    bp, seq, _ = x_prompt.shape
    n_new = x_sample.shape[1]
    n_past = page_table.shape[1] * cache_k.shape[2]
    hp = jnp.concatenate([jnp.broadcast_to(meta.astype(dt)[None], (bp, N_META, D_MODEL)), x_prompt], axis=1)
    hs = x_sample
    pos_p = jnp.arange(hp.shape[1], dtype=jnp.int32)
    pos_s = n_past + jnp.arange(n_new, dtype=jnp.int32)
    ksel_p = min(TOPK_MAX, seq // 4)
    ksel_s = min(TOPK_MAX, (n_past + n_new) // 4)
    pad_p = (-N_META) % CHUNK
    chunk_s = min(CHUNK, n_new)
    lb_soft = jax.nn.softmax(hg_lb.astype(jnp.float32), axis=0)
    lb_all = jnp.cumsum(lb_soft, axis=0) - lb_soft[0]
    kp, vp, ikp, cp, dp, hgp = [], [], [], [], [], []
    ks, vs, iks, cs, ds, hgs = [], [], [], [], [], []
    for l in range(DEPTH):
        if l % 2 == 0:
            i = l // 2
            ew = (w_in_e[i], w_out_e[i], dn_conv[i], dn_a_log[i], dn_dt_bias[i], dn_norm[i])
            empty_kv = jnp.zeros((bp, 0, A_KV_HEADS, A_HEAD_DIM), dt)
            empty_ik = jnp.zeros((bp, 0, IDX_DIM), dt)
            conv0 = jnp.zeros((bp, CONV_W - 1, CONV_DIM), dt)
            s0 = jnp.zeros((bp, DN_HEADS, DN_DK, DN_DV), jnp.float32)
            out, k_, v_, ik_, buf, st = _even_mixer(_rmsnorm(hp, ln_mix[l]), pos_p, empty_kv, empty_kv, empty_ik,
                                                    conv0, s0, ksel_p, pad_p, CHUNK, *ew)
            hp = hp + out
            kp.append(k_); vp.append(v_); ikp.append(ik_); cp.append(buf); dp.append(st)
            out, k_, v_, ik_, buf, st = _even_mixer(_rmsnorm(hs, ln_mix[l]), pos_s,
                                                    _gather_pages(cache_k[i], page_table),
                                                    _gather_pages(cache_v[i], page_table),
                                                    _gather_pages(cache_ik[i], page_table),
                                                    state_conv[i], state_delta[i], ksel_s, 0, chunk_s, *ew)
            hs = hs + out
            ks.append(k_); vs.append(v_); iks.append(ik_); cs.append(buf); ds.append(st)
        else:
            j = l // 2
            s0 = jnp.zeros((bp, HG_HEADS, HG_DK, HG_DV), jnp.float32)
            out, st = _odd_mixer(_rmsnorm(hp, ln_mix[l]), s0, lb_all[l], pad_p, CHUNK, w_in_o[j], w_out_o[j], hg_norm[j])
            hp = hp + out
            hgp.append(st)
            out, st = _odd_mixer(_rmsnorm(hs, ln_mix[l]), state_hgrn[j], lb_all[l], 0, chunk_s, w_in_o[j], w_out_o[j], hg_norm[j])
            hs = hs + out
            hgs.append(st)
        hp = hp + _sq_relu_mlp(_rmsnorm(hp, ln_ffn[l]), w_up[l], w_down[l])
        hs = hs + _sq_relu_mlp(_rmsnorm(hs, ln_ffn[l]), w_up[l], w_down[l])
    y_prompt = _rmsnorm(hp, ln_final)[:, N_META:]
    y_sample = _rmsnorm(hs, ln_final)
    return (y_prompt, y_sample,
            jnp.stack(kp).astype(cache_k.dtype), jnp.stack(vp).astype(cache_v.dtype), jnp.stack(ikp).astype(cache_ik.dtype),
            jnp.stack(cp).astype(state_conv.dtype), jnp.stack(dp).astype(state_delta.dtype), jnp.stack(hgp).astype(state_hgrn.dtype),
            jnp.stack(ks).astype(cache_k.dtype), jnp.stack(vs).astype(cache_v.dtype), jnp.stack(iks).astype(cache_ik.dtype),
            jnp.stack(cs).astype(state_conv.dtype), jnp.stack(ds).astype(state_delta.dtype), jnp.stack(hgs).astype(state_hgrn.dtype))
```

```python
import functools

import jax
import jax.numpy as jnp
from jax import lax
from jax.experimental import pallas as pl
from jax.experimental.pallas import tpu as pltpu

F32 = jnp.float32
BF16 = jnp.bfloat16
I32 = jnp.int32
HI = lax.Precision.HIGHEST

D_MODEL = 1024
N_META = 16
PAD_FRONT = 112
A_HEAD_DIM = 64
A_HEADS = 8
A_KV_HEADS = 2
A_REP = A_HEADS // A_KV_HEADS
IDX_HEADS = 4
IDX_DIM = 64
TOPK_MAX = 256
META_BONUS = 1e30
ROPE_THETA = 500000.0
ROPE_HALF = 8
DN_HEADS = 4
DN_D = 128
CONV_W = 4
CONV_DIM = 1536
HG_HEADS = 8
HG_D = 128
D_FF = 4096
EPS = 1e-6

C_AQ, C_AK, C_IQ, C_IKW, C_AV, C_DQKV, C_DZ, C_GATE, C_END = 0, 512, 640, 896, 1024, 1152, 2688, 3200, 3328
GATE_B, GATE_A = 0, 4
IW_LANE = 64

NEG = -1e30
INT_MIN = -(2 ** 31)
MINKEY = INT_MIN + 0x7FFFFF
VMEM_LIMIT = 56 * 1024 * 1024
LANES = 128


def _dot(a, b, prec=None):
    return lax.dot_general(a, b, (((1,), (0,)), ((), ())), precision=prec, preferred_element_type=F32)


def _dot_nt(a, b, prec=None):
    return lax.dot_general(a, b, (((1,), (1,)), ((), ())), precision=prec, preferred_element_type=F32)


def _dot_tn(a, b, prec=None):
    return lax.dot_general(a, b, (((0,), (0,)), ((), ())), precision=prec, preferred_element_type=F32)


def _rms(x, g):
    return x * lax.rsqrt(jnp.mean(x * x, axis=-1, keepdims=True) + EPS) * g


def _silu(x):
    return x * jax.nn.sigmoid(x)


def _softplus(x):
    return jnp.maximum(x, 0.0) + jnp.log1p(jnp.exp(-jnp.abs(x)))


def _iota2(shape, dim):
    return lax.broadcasted_iota(I32, shape, dim)


def _params(sem):
    return pltpu.CompilerParams(dimension_semantics=sem, vmem_limit_bytes=VMEM_LIMIT)


def _split2(x):
    hi = x.astype(BF16).astype(F32)
    return hi, x - hi


def _split3_cat(x):
    hi = x.astype(BF16).astype(F32)
    r = x - hi
    mid = r.astype(BF16).astype(F32)
    return jnp.concatenate([hi, mid, r - mid], axis=1).astype(BF16)


def _mask_dot3(mask, x3):
    n = x3.shape[1] // 3
    y = _dot(mask, x3)
    return y[:, 0:n] + y[:, n:2 * n] + y[:, 2 * n:3 * n]


def _dot_x3(a, b):
    ah, al = _split2(a)
    bh, bl = _split2(b)
    lhs = jnp.concatenate([ah, ah, al], axis=1).astype(BF16)
    rhs = jnp.concatenate([bh, bl, bh], axis=0).astype(BF16)
    return _dot(lhs, rhs)


def _proj_even_kernel(x_ref, g_ref, w_ref, tab_ref, q_ref, k_ref, iq4_ref, ikw_ref, ik4_ref, v_ref, dqkv_ref,
                      dz_ref, gate_ref, kb_ref, vp_ref):
    xn = _rms(x_ref[...], g_ref[...]).astype(BF16)

    def mm(a, b):
        return jnp.dot(xn, w_ref[:, a:b], preferred_element_type=F32)

    def rope(y, off):
        cos = tab_ref[0, :, off:off + LANES]
        s_lo = tab_ref[1, :, off:off + LANES]
        s_hi = tab_ref[2, :, off:off + LANES]
        outs = []
        for b in range(y.shape[1] // LANES):
            yb = y[:, b * LANES:(b + 1) * LANES]
            outs.append(yb * cos + pltpu.roll(yb, LANES - ROPE_HALF, 1) * s_lo + pltpu.roll(yb, ROPE_HALF, 1) * s_hi)
        return outs[0] if len(outs) == 1 else jnp.concatenate(outs, axis=1)

    q_ref[...] = rope(mm(C_AQ, C_AK), 0) * (A_HEAD_DIM ** -0.5)
    k = rope(mm(C_AK, C_IQ), 0)
    k_ref[...] = k
    kb_ref[...] = k.astype(BF16)
    hi, lo = _split2(rope(mm(C_IQ, C_IKW), 0))
    pieces = []
    for h in range(IDX_HEADS):
        a, b = hi[:, h * IDX_DIM:(h + 1) * IDX_DIM], lo[:, h * IDX_DIM:(h + 1) * IDX_DIM]
        pieces += [a, b, a, b]
    iq4_ref[...] = jnp.concatenate(pieces, axis=1).astype(BF16)
    ikw = rope(mm(C_IKW, C_AV), LANES)
    ikw_ref[...] = ikw
    hi, lo = _split2(ikw[:, 0:IDX_DIM])
    ik4_ref[...] = jnp.concatenate([hi, hi, lo, lo], axis=1).astype(BF16)
    v = mm(C_AV, C_DQKV)
    v_ref[...] = v
    one = jnp.ones((v.shape[0], A_HEAD_DIM), F32)
    vp_ref[...] = jnp.concatenate([v[:, 0:A_HEAD_DIM], one, v[:, A_HEAD_DIM:], one], axis=1).astype(BF16)
    dqkv_ref[...] = mm(C_DQKV, C_DZ)
    dz_ref[...] = mm(C_DZ, C_GATE)
    gate_ref[...] = mm(C_GATE, C_END)


def _proj_even(x, g, w, tab, tm):
    t = x.shape[0]
    nt = tab.shape[1] // tm
    widths = (512, 128, 4 * IDX_HEADS * IDX_DIM, 128, 4 * IDX_DIM, 128, CONV_DIM, 512, 128, 128, 256)
    dtypes = (F32, F32, BF16, F32, BF16, F32, F32, F32, F32, BF16, BF16)
    return pl.pallas_call(
        _proj_even_kernel,
        grid=(t // tm,),
        in_specs=[
            pl.BlockSpec((tm, D_MODEL), lambda i: (i, 0)),
            pl.BlockSpec((1, D_MODEL), lambda i: (0, 0)),
            pl.BlockSpec((D_MODEL, C_END), lambda i: (0, 0)),
            pl.BlockSpec((3, tm, 2 * LANES), lambda i: (0, i % nt, 0)),
        ],
        out_specs=[pl.BlockSpec((tm, wd), lambda i: (i, 0)) for wd in widths],
        out_shape=[jax.ShapeDtypeStruct((t, wd), dt) for wd, dt in zip(widths, dtypes)],
        compiler_params=_params(("parallel",)),
        name="proj_even",
    )(x, g, w, tab)


def _rope_tables(pos):
    inv = ROPE_THETA ** (-jnp.arange(ROPE_HALF, dtype=F32) / ROPE_HALF)
    ang = pos.astype(F32)[:, None] * inv[None, :]
    cos, sin = jnp.cos(ang), jnp.sin(ang)
    lane = jnp.arange(LANES)
    l64 = lane % A_HEAD_DIM
    idx = l64 % ROPE_HALF
    lo = l64 < ROPE_HALF
    hi = (l64 >= ROPE_HALF) & (l64 < 2 * ROPE_HALF)
    c = jnp.where(lo | hi, cos[:, idx], 1.0)
    s_lo = jnp.where(lo, -sin[:, idx], 0.0)
    s_hi = jnp.where(hi, sin[:, idx], 0.0)
    first = lane < A_HEAD_DIM
    c2 = jnp.where(first, c, 1.0)
    s_lo2 = jnp.where(first, s_lo, 0.0)
    s_hi2 = jnp.where(first, s_hi, 0.0)
    return jnp.stack([jnp.concatenate([c, c2], 1), jnp.concatenate([s_lo, s_lo2], 1),
                      jnp.concatenate([s_hi, s_hi2], 1)])


def _loop(n, body, init):
    if isinstance(n, int):
        carry = init
        for j in range(n):
            carry = body(j, carry)
        return carry
    return lax.fori_loop(0, n, body, init)


def _key_to_float(key):
    return lax.bitcast_convert_type(jnp.where(key < 0, key ^ 0x7FFFFFFF, key), F32)


def _dsa_core(q_all, w_cols, qpos, score_fn, qk_fn, pv_fn, kidx_fn, n_k, kc_w, k_sel, tri_ref,
              key_scr, bias_scr, m_scr, acc_scr):
    mq = q_all.shape[0]
    n_sub = kc_w // LANES
    ksel = float(k_sel)

    def score_body(j, carry):
        s_all = score_fn(j)
        score = w_cols[0] * jnp.maximum(s_all[0:mq], 0.0)
        for h in range(1, IDX_HEADS):
            score = score + w_cols[h] * jnp.maximum(s_all[h * mq:(h + 1) * mq], 0.0)
        kidx = kidx_fn(j)
        score = jnp.where(kidx < N_META, META_BONUS, score)
        visible = (kidx >= 0) & (kidx <= qpos)
        key_scr[j] = jnp.where(visible, score, -jnp.inf)
        return carry

    _loop(n_k, score_body, 0)

    def count_ge(cand):
        def body(j, acc):
            kc = key_scr[j]
            for c in range(n_sub):
                acc = acc + jnp.where(kc[:, c * LANES:(c + 1) * LANES] >= cand, 1.0, 0.0)
            return acc

        acc = _loop(n_k, body, jnp.zeros((mq, LANES), F32))
        return jnp.sum(acc, axis=1, keepdims=True)

    zero_b = jnp.zeros((mq, LANES), I32)
    base = jnp.where(count_ge(jnp.zeros((mq, LANES), F32)) >= ksel, zero_b, zero_b + INT_MIN)

    def bit_body(t, base):
        cand = base | lax.shift_left(jnp.int32(1), 30 - t)
        cnt = jnp.where(cand <= MINKEY, ksel, count_ge(_key_to_float(cand)))
        return jnp.where(cnt >= ksel, cand, base)

    thr_key = lax.fori_loop(0, 31, bit_body, base)
    thr = _key_to_float(thr_key)
    nxt = _key_to_float(thr_key + 1)

    need = ksel - count_ge(nxt)
    tri = tri_ref[...]
    ones = jnp.ones((LANES, LANES), BF16)

    def select_body(j, before):
        kc = key_scr[j]
        biases = []
        for c in range(n_sub):
            blk = kc[:, c * LANES:(c + 1) * LANES]
            above = blk >= nxt
            at_least = blk >= thr
            tie_b = jnp.where(above, 0.0, jnp.where(at_least, 1.0, 0.0)).astype(BF16)
            rank = before + _dot(tie_b, tri)
            tie_bias = jnp.where(rank < need, 0.0, NEG)
            bias = jnp.where(above, 0.0, jnp.where(at_least, tie_bias, NEG))
            biases.append(jnp.where(blk == -jnp.inf, NEG, bias))
            before = before + _dot(tie_b, ones)
        bias_scr[j] = biases[0] if n_sub == 1 else jnp.concatenate(biases, axis=1)
        return before

    _loop(n_k, select_body, jnp.zeros((mq, LANES), F32))

    q_groups = [jnp.concatenate(
        [q_all[:, (g * A_REP + r) * A_HEAD_DIM:(g * A_REP + r + 1) * A_HEAD_DIM] for r in range(A_REP)],
        axis=0).astype(BF16) for g in range(A_KV_HEADS)]
    m_scr[...] = jnp.full(m_scr.shape, float(jnp.finfo(BF16).min), F32)
    acc_scr[...] = jnp.zeros(acc_scr.shape, F32)

    def attn_body(j, carry):
        bias4 = jnp.concatenate([bias_scr[j]] * A_REP, axis=0).astype(BF16)
        for g in range(A_KV_HEADS):
            sb = qk_fn(j, g, q_groups[g]).astype(BF16) + bias4
            smax = sb[:, 0:LANES]
            for c in range(1, n_sub):
                smax = jnp.maximum(smax, sb[:, c * LANES:(c + 1) * LANES])
            m_prev = m_scr[g]
            m_new = jnp.maximum(m_prev, jnp.max(smax.astype(F32), axis=1, keepdims=True))
            m_b = m_new.astype(BF16)
            ps = [jnp.exp(sb[:, c * LANES:(c + 1) * LANES] - m_b) for c in range(n_sub)]
            p = ps[0] if n_sub == 1 else jnp.concatenate(ps, axis=1)
            acc_scr[g] = acc_scr[g] * jnp.exp(m_prev - m_new) + pv_fn(j, g, p)
            m_scr[g] = m_new
        return carry

    _loop(n_k, attn_body, 0)

    pieces = []
    for g in range(A_KV_HEADS):
        acc = acc_scr[g]
        o = acc[:, 0:A_HEAD_DIM] / acc[:, A_HEAD_DIM:2 * A_HEAD_DIM]
        pieces += [o[r * mq:(r + 1) * mq, :] for r in range(A_REP)]
    out = jnp.concatenate(pieces, axis=1)
    return jnp.where(qpos >= 0, out, 0.0)


def _dsa_prompt_kernel(q_ref, iq4_ref, ikwq_ref, kb_ref, vp_ref, ik4_ref, tri_ref, o_ref,
                       key_scr, bias_scr, m_scr, acc_scr, *, qb, kc_w, k_sel):
    row0 = pl.program_id(1) * qb
    qpos = row0 + _iota2((qb, 1), 0) - PAD_FRONT
    n_k = (row0 + qb + kc_w - 1) // kc_w
    scale = (IDX_HEADS * IDX_DIM) ** -0.5
    w_cols = [ikwq_ref[:, IW_LANE + h:IW_LANE + h + 1] * scale for h in range(IDX_HEADS)]
    wq = 4 * IDX_DIM
    iq4 = jnp.concatenate([iq4_ref[:, h * wq:(h + 1) * wq] for h in range(IDX_HEADS)], axis=0)

    def score_fn(j):
        off = pl.multiple_of(j * kc_w, kc_w)
        return _dot_nt(iq4, ik4_ref[pl.ds(off, kc_w), :])

    def qk_fn(j, g, qg):
        off = pl.multiple_of(j * kc_w, kc_w)
        return _dot_nt(qg, kb_ref[pl.ds(off, kc_w), g * A_HEAD_DIM:(g + 1) * A_HEAD_DIM])

    def pv_fn(j, g, p):
        off = pl.multiple_of(j * kc_w, kc_w)
        return _dot(p, vp_ref[pl.ds(off, kc_w), g * LANES:(g + 1) * LANES])

    def kidx_fn(j):
        return j * kc_w + _iota2((1, kc_w), 1) - PAD_FRONT

    o_ref[...] = _dsa_core(q_ref[...], w_cols, qpos, score_fn, qk_fn, pv_fn, kidx_fn, n_k, kc_w, k_sel, tri_ref,
                           key_scr, bias_scr, m_scr, acc_scr)


def _tri_const():
    a = jnp.arange(LANES)
    return (a[:, None] < a[None, :]).astype(BF16)


def _dsa_scratch(mq, n_chunks, kc_w):
    return [
        pltpu.VMEM((n_chunks, mq, kc_w), F32),
        pltpu.VMEM((n_chunks, mq, kc_w), F32),
        pltpu.VMEM((A_KV_HEADS, A_REP * mq, LANES), F32),
        pltpu.VMEM((A_KV_HEADS, A_REP * mq, LANES), F32),
    ]


def _dsa_prompt(q, iq4, ikw, ik4, kb, vp, k_sel, qb=128, kc_w=384):
    b, length, _ = q.shape
    n_chunks = length // kc_w

    def blk(wd):
        return pl.BlockSpec((None, qb, wd), lambda bi, i: (bi, i, 0))

    def full(wd):
        return pl.BlockSpec((None, length, wd), lambda bi, i: (bi, 0, 0))

    return pl.pallas_call(
        functools.partial(_dsa_prompt_kernel, qb=qb, kc_w=kc_w, k_sel=k_sel),
        grid=(b, length // qb),
        in_specs=[blk(512), blk(iq4.shape[2]), blk(128), full(kb.shape[2]), full(vp.shape[2]), full(ik4.shape[2]),
                  pl.BlockSpec((LANES, LANES), lambda bi, i: (0, 0))],
        out_specs=blk(512),
        out_shape=jax.ShapeDtypeStruct((b, length, 512), F32),
        scratch_shapes=_dsa_scratch(qb, n_chunks, kc_w),
        compiler_params=_params(("parallel", "arbitrary")),
        name="dsa_prompt",
    )(q, iq4, ikw, kb, vp, ik4, _tri_const())


def _dsa_sample_kernel(pt_ref, q_ref, iq4_ref, ikwq_ref, ik4new_ref, knew_ref, vnew_ref, *rest, n_pages, page, n_new,
                       k_sel):
    kpages = rest[0:n_pages]
    vpages = rest[n_pages:2 * n_pages]
    ikpages = rest[2 * n_pages:3 * n_pages]
    tri_ref, o_ref, kt_buf, vt_buf, ik4t_buf, key_scr, bias_scr, m_scr, acc_scr = rest[3 * n_pages:]
    n_past = n_pages * page
    n_keys = kt_buf.shape[2]
    for p in range(n_pages):
        cols = slice(p * page, (p + 1) * page)
        kt_buf[:, :, cols] = kpages[p][...].astype(BF16)
        vt_buf[:, 0:A_HEAD_DIM, cols] = vpages[p][...].astype(BF16)
        hi, lo = _split2(ikpages[p][...])
        ik4t_buf[:, cols] = jnp.concatenate([hi, hi, lo, lo], axis=0).astype(BF16)
    vt_buf[:, A_HEAD_DIM:, :] = jnp.ones((A_KV_HEADS, LANES - A_HEAD_DIM, n_keys), BF16)

    def new_block_t(rows_ref):
        x = rows_ref[...].astype(F32)
        return jnp.concatenate([x, jnp.zeros((LANES - n_new, x.shape[1]), F32)], axis=0).T

    tail = slice(n_past, n_keys)
    kt_new, vt_new = new_block_t(knew_ref), new_block_t(vnew_ref)
    for g in range(A_KV_HEADS):
        kt_buf[g, :, tail] = kt_new[g * A_HEAD_DIM:(g + 1) * A_HEAD_DIM, :].astype(BF16)
        vt_buf[g, 0:A_HEAD_DIM, tail] = vt_new[g * A_HEAD_DIM:(g + 1) * A_HEAD_DIM, :].astype(BF16)
    ik4t_buf[:, tail] = new_block_t(ik4new_ref).astype(BF16)

    qpos = n_past + _iota2((n_new, 1), 0)
    scale = (IDX_HEADS * IDX_DIM) ** -0.5
    w_cols = [ikwq_ref[:, IW_LANE + h:IW_LANE + h + 1] * scale for h in range(IDX_HEADS)]
    wq = 4 * IDX_DIM
    iq4_f = iq4_ref[...].astype(F32)
    iq4 = jnp.concatenate([iq4_f[:, h * wq:(h + 1) * wq] for h in range(IDX_HEADS)], axis=0).astype(BF16)

    def score_fn(j):
        return _dot(iq4, ik4t_buf[...])

    def qk_fn(j, g, qg):
        return _dot(qg, kt_buf[g])

    def pv_fn(j, g, p):
        return _dot_nt(p, vt_buf[g])

    def kidx_fn(j):
        kidx = _iota2((1, n_keys), 1)
        return jnp.where(kidx >= n_past + n_new, 2 ** 30, kidx)

    o_ref[...] = _dsa_core(q_ref[...], w_cols, qpos, score_fn, qk_fn, pv_fn, kidx_fn, 1, n_keys, k_sel, tri_ref,
                           key_scr, bias_scr, m_scr, acc_scr)


def _dsa_sample(q, iq4, ikw, ik4, k, v, pool_kt, pool_vt, pool_ikt, layer, page_table, k_sel):
    b, n_new, _ = q.shape
    n_pages = page_table.shape[1]
    page = pool_kt.shape[4]
    n_keys = n_pages * page + LANES

    def blk(wd):
        return pl.BlockSpec((None, n_new, wd), lambda bi, pt: (bi, 0, 0))

    def kv_page(p):
        return pl.BlockSpec((None, None, A_KV_HEADS, A_HEAD_DIM, page),
                            lambda bi, pt, p=p: (layer, pt[bi * n_pages + p], 0, 0, 0))

    def ik_page(p):
        return pl.BlockSpec((None, None, IDX_DIM, page), lambda bi, pt, p=p: (layer, pt[bi * n_pages + p], 0, 0))

    in_specs = [blk(512), blk(iq4.shape[2]), blk(128), blk(ik4.shape[2]), blk(128), blk(128)]
    in_specs += [kv_page(p) for p in range(n_pages)]
    in_specs += [kv_page(p) for p in range(n_pages)]
    in_specs += [ik_page(p) for p in range(n_pages)]
    in_specs += [pl.BlockSpec((LANES, LANES), lambda bi, pt: (0, 0))]
    scratch = [pltpu.VMEM((A_KV_HEADS, A_HEAD_DIM, n_keys), BF16), pltpu.VMEM((A_KV_HEADS, LANES, n_keys), BF16),
               pltpu.VMEM((4 * IDX_DIM, n_keys), BF16)] + _dsa_scratch(n_new, 1, n_keys)
    grid_spec = pltpu.PrefetchScalarGridSpec(
        num_scalar_prefetch=1, grid=(b,), in_specs=in_specs, out_specs=blk(512), scratch_shapes=scratch)
    return pl.pallas_call(
        functools.partial(_dsa_sample_kernel, n_pages=n_pages, page=page, n_new=n_new, k_sel=k_sel),
        grid_spec=grid_spec,
        out_shape=jax.ShapeDtypeStruct((b, n_new, 512), F32),
        compiler_params=_params(("arbitrary",)),
        name="dsa_sample",
    )(page_table.reshape(-1), q, iq4, ikw, ik4, k, v, *([pool_kt] * n_pages), *([pool_vt] * n_pages),
      *([pool_ikt] * n_pages), _tri_const())


def _block_masks(c, blk):
    t = _iota2((c, c), 0)
    s = _iota2((c, c), 1)
    sh = blk.bit_length() - 1
    same = (t >> sh) == (s >> sh)
    incl = same & (s <= t)
    strict = same & (s < t)
    upper = same & (s > t)
    return incl, strict, upper


def _conv_silu(full_ref, base, c, cw):
    acc = full_ref[pl.ds(base, c), :] * cw[CONV_W - 1:CONV_W, :]
    for s in range(1, CONV_W):
        acc = acc + full_ref[pl.ds(base - s, c), :] * cw[CONV_W - 1 - s:CONV_W - s, :]
    return _silu(acc)


def _l2n(x):
    return x * lax.rsqrt(jnp.sum(x * x, axis=-1, keepdims=True) + EPS)


def _delta_chunk(conv, gate, dz, alog, dtb, dnorm, s_get, s_set, c, blk):
    incl, strict, upper = _block_masks(c, blk)
    lower_i = jnp.where(incl, 1.0, 0.0).astype(BF16)
    lower_s = jnp.where(strict, 1.0, 0.0)
    upper_s = jnp.where(upper, 1.0, 0.0).astype(BF16)
    eye = jnp.where(_iota2((c, c), 0) == _iota2((c, c), 1), 1.0, 0.0)
    n_blk = c // blk
    n_sq = max(blk.bit_length() - 2, 0)

    beta_all = jax.nn.sigmoid(gate)
    g_all = -jnp.exp(alog) * _softplus(gate + dtb)
    g3 = _split3_cat(g_all)
    gam_all = _mask_dot3(lower_i, g3)
    post_all = _mask_dot3(upper_s, g3)
    outs = []
    for h in range(DN_HEADS):
        q = _l2n(conv[:, h * DN_D:(h + 1) * DN_D]) * (DN_D ** -0.5)
        k = _l2n(conv[:, 512 + h * DN_D:512 + (h + 1) * DN_D])
        v = conv[:, 1024 + h * DN_D:1024 + (h + 1) * DN_D]
        lane = GATE_A + h
        gb = jnp.broadcast_to(g_all[:, lane:lane + 1], (c, LANES))
        bb = jnp.broadcast_to(beta_all[:, GATE_B + h:GATE_B + h + 1], (c, LANES))
        gam = jnp.broadcast_to(gam_all[:, lane:lane + 1], (c, LANES))
        post = jnp.broadcast_to(post_all[:, lane:lane + 1], (c, LANES))
        diff = _mask_dot3(lower_i, _split3_cat(gb[:, 0:c] * lower_s))
        dec = jnp.where(incl, jnp.exp(diff), 0.0)
        kkt = _dot_nt(k, k)
        n = jnp.where(strict, -(bb[:, 0:c] * kkt * dec), 0.0)
        t_inv = eye + n
        pw = n
        for _ in range(n_sq):
            pw = _dot_x3(pw, pw)
            t_inv = t_inv + _dot_x3(t_inv, pw)
        egam = jnp.exp(gam)
        uw = _dot_x3(t_inv, jnp.concatenate([v * bb, k * (bb * egam)], axis=1))
        u, w = uw[:, 0:DN_D], uw[:, DN_D:2 * DN_D]
        qe = q * egam
        kpost = k * jnp.exp(post)
        v_new, o_inter = [], []
        for e in range(n_blk):
            rows = slice(e * blk, (e + 1) * blk)
            s_old = s_get(h, e)
            vn = u[rows] - _dot(w[rows], s_old)
            v_new.append(vn)
            o_inter.append(_dot(qe[rows], s_old))
            gl = gam[(e + 1) * blk - 1:(e + 1) * blk, :]
            s_set(h, e, s_old * jnp.exp(gl) + _dot_tn(kpost[rows], vn))
        v_new = v_new[0] if n_blk == 1 else jnp.concatenate(v_new, axis=0)
        o_inter = o_inter[0] if n_blk == 1 else jnp.concatenate(o_inter, axis=0)
        attn = _dot_nt(q, k) * dec
        o = o_inter + _dot(attn, v_new)
        outs.append(_rms(o, dnorm) * _silu(dz[:, h * DN_D:(h + 1) * DN_D]))
    return jnp.concatenate(outs, axis=1)


def _delta_prompt_kernel(dqkv_ref, gate_ref, dz_ref, cw_ref, alog_ref, dtb_ref, dnorm_ref, o_ref, s_out_ref,
                         full_scr, s_scr, *, c, blk):
    ci = pl.program_id(1)

    @pl.when(ci == 0)
    def _():
        full_scr[0:8, :] = jnp.zeros((8, CONV_DIM), F32)
        s_scr[...] = jnp.zeros(s_scr.shape, F32)

    full_scr[8:8 + c, :] = dqkv_ref[...]
    conv = _conv_silu(full_scr, 8, c, cw_ref[...])
    full_scr[0:8, :] = full_scr[c:c + 8, :]

    def s_get(h, e):
        return s_scr[h]

    def s_set(h, e, val):
        s_scr[h] = val

    o_ref[...] = _delta_chunk(conv, gate_ref[...], dz_ref[...], alog_ref[...], dtb_ref[...], dnorm_ref[...],
                              s_get, s_set, c, blk)

    @pl.when(ci == pl.num_programs(1) - 1)
    def _():
        s_out_ref[...] = s_scr[...]


def _delta_prompt(dqkv, gate, dz, cw, alog, dtb, dnorm, c=128, blk=64):
    b, length, _ = dqkv.shape
    nch = length // c

    def rows(wd):
        return pl.BlockSpec((None, c, wd), lambda bi, ci: (bi, ci, 0))

    def par(r, wd):
        return pl.BlockSpec((r, wd), lambda bi, ci: (0, 0))

    return pl.pallas_call(
        functools.partial(_delta_prompt_kernel, c=c, blk=blk),
        grid=(b, nch),
        in_specs=[rows(CONV_DIM), rows(128), rows(512), par(CONV_W, CONV_DIM), par(1, 128), par(1, 128),
                  par(1, 128)],
        out_specs=[rows(512), pl.BlockSpec((None, DN_HEADS, DN_D, DN_D), lambda bi, ci: (bi, 0, 0, 0))],
        out_shape=[jax.ShapeDtypeStruct((b, length, 512), F32),
                   jax.ShapeDtypeStruct((b, DN_HEADS, DN_D, DN_D), F32)],
        scratch_shapes=[pltpu.VMEM((c + 8, CONV_DIM), F32), pltpu.VMEM((DN_HEADS, DN_D, DN_D), F32)],
        compiler_params=_params(("parallel", "arbitrary")),
        name="delta_prompt",
    )(dqkv, gate, dz, cw, alog, dtb, dnorm)


def _delta_sample_kernel(dqkv_ref, gate_ref, dz_ref, buf_ref, s_in_ref, cw_ref, alog_ref, dtb_ref, dnorm_ref,
                         o_ref, s_out_ref, full_scr, *, g, n_new):
    cw = cw_ref[...]
    convs = []
    for e in range(g):
        full_scr[e, 8 - (CONV_W - 1):8, :] = buf_ref[e]
        full_scr[e, 8:8 + n_new, :] = dqkv_ref[e * n_new:(e + 1) * n_new, :]
        convs.append(_conv_silu(full_scr.at[e], 8, n_new, cw))
    conv = jnp.concatenate(convs, axis=0)

    def s_get(h, e):
        return s_in_ref[e, h]

    def s_set(h, e, val):
        s_out_ref[e, h] = val

    o_ref[...] = _delta_chunk(conv, gate_ref[...], dz_ref[...], alog_ref[...], dtb_ref[...], dnorm_ref[...],
                              s_get, s_set, g * n_new, n_new)


def _delta_sample(dqkv, gate, dz, buf, s_in, layer, cw, alog, dtb, dnorm, n_new, g=8):
    b = buf.shape[1]
    c = g * n_new

    def blk(wd):
        return pl.BlockSpec((c, wd), lambda i: (i, 0))

    def par(r, wd):
        return pl.BlockSpec((r, wd), lambda i: (0, 0))

    st = pl.BlockSpec((g, DN_HEADS, DN_D, DN_D), lambda i: (i, 0, 0, 0))
    st_in = pl.BlockSpec((None, g, DN_HEADS, DN_D, DN_D), lambda i: (layer, i, 0, 0, 0))
    return pl.pallas_call(
        functools.partial(_delta_sample_kernel, g=g, n_new=n_new),
        grid=(b // g,),
        in_specs=[blk(CONV_DIM), blk(128), blk(512),
                  pl.BlockSpec((None, g, CONV_W - 1, CONV_DIM), lambda i: (layer, i, 0, 0)), st_in,
                  par(CONV_W, CONV_DIM), par(1, 128), par(1, 128), par(1, 128)],
        out_specs=[blk(512), st],
        out_shape=[jax.ShapeDtypeStruct((b * n_new, 512), F32), jax.ShapeDtypeStruct(s_in.shape[1:], F32)],
        scratch_shapes=[pltpu.VMEM((g, 8 + n_new, CONV_DIM), F32)],
        compiler_params=_params(("parallel",)),
        name="delta_sample",
    )(dqkv, gate, dz, buf, s_in, cw, alog, dtb, dnorm)


def _hgrn_chunk(qraw, fraw, v, gz, loglb, log1mlb, omlb, hnorm, s_get, s_set, c, blk):
    incl, strict, upper = _block_masks(c, blk)
    lower_i = jnp.where(incl, 1.0, 0.0).astype(BF16)
    upper_s = jnp.where(upper, 1.0, 0.0).astype(BF16)
    t = _iota2((c, c), 0)
    s = _iota2((c, c), 1)
    trow = _iota2((c, 1), 0)
    eye = t == s
    n_blk = c // blk

    q = _silu(qraw)
    lsig = jnp.minimum(fraw, 0.0) - jnp.log1p(jnp.exp(-jnp.abs(fraw)))
    a = loglb
    b = log1mlb + lsig
    logf = jnp.maximum(a, b) + jnp.log1p(jnp.exp(-jnp.abs(a - b)))
    kk = omlb * jax.nn.sigmoid(-fraw)

    logf3 = _split3_cat(logf)
    cb = _mask_dot3(lower_i, logf3)
    post = _mask_dot3(upper_s, logf3)
    qe = q * jnp.exp(cb)
    kpost = kk * jnp.exp(post)

    levels = []
    hs = blk // 2
    while hs >= 1:
        sh = hs.bit_length() - 1
        second_t = ((t >> sh) & 1) == 1
        mid_t = ((t >> (sh + 1)) << (sh + 1)) + hs - 1
        sel_hi = jnp.where((s > mid_t) & (s <= t), 1.0, 0.0)
        sel_lo = jnp.where((s > t) & (s <= mid_t), 1.0, 0.0)
        arg = _mask_dot3(jnp.where(second_t, sel_hi, sel_lo).astype(BF16), logf3)
        second_row = ((trow >> sh) & 1) == 1
        x = jnp.where(second_row, q, kk) * jnp.exp(arg)
        pair = ((t >> (sh + 1)) == (s >> (sh + 1))) & second_t & (((s >> sh) & 1) == 0)
        levels.append((x, pair))
        hs //= 2

    ones_blk = jnp.ones((blk, LANES), F32)
    outs = []
    for h in range(HG_HEADS):
        ln = slice(h * HG_D, (h + 1) * HG_D)
        attn = jnp.where(eye, _dot_nt(q[:, ln], kk[:, ln]), 0.0)
        for x, pair in levels:
            attn = attn + jnp.where(pair, _dot_nt(x[:, ln], x[:, ln]), 0.0)
        o_inter = []
        for e in range(n_blk):
            rows = slice(e * blk, (e + 1) * blk)
            s_old = s_get(h, e)
            o_inter.append(_dot(qe[rows, ln], s_old))
            cl_col = _dot_tn(logf[rows, ln], ones_blk, HI)
            s_set(h, e, s_old * jnp.exp(cl_col) + _dot_tn(kpost[rows, ln], v[rows, ln]))
        o_inter = o_inter[0] if n_blk == 1 else jnp.concatenate(o_inter, axis=0)
        o = o_inter + _dot(attn, v[:, ln])
        outs.append(_rms(o, hnorm) * _silu(gz[:, ln]))
    return jnp.concatenate(outs, axis=1)


def _hgrn_prompt_kernel(q_ref, f_ref, v_ref, gz_ref, loglb_ref, log1mlb_ref, omlb_ref, hnorm_ref, o_ref, s_out_ref,
                        s_scr, *, c, blk):
    ci = pl.program_id(1)

    @pl.when(ci == 0)
    def _():
        s_scr[...] = jnp.zeros(s_scr.shape, F32)

    def s_get(h, e):
        return s_scr[h]

    def s_set(h, e, val):
        s_scr[h] = val

    o_ref[...] = _hgrn_chunk(q_ref[...], f_ref[...], v_ref[...], gz_ref[...], loglb_ref[...], log1mlb_ref[...],
                             omlb_ref[...], hnorm_ref[...], s_get, s_set, c, blk)

    @pl.when(ci == pl.num_programs(1) - 1)
    def _():
        s_out_ref[...] = s_scr[...]


def _hgrn_prompt(proj, loglb, log1mlb, omlb, hnorm, c=128, blk=64):
    b, length, _ = proj.shape

    def col(j):
        return pl.BlockSpec((None, c, D_MODEL), lambda bi, ci, j=j: (bi, ci, j))

    def par(wd):
        return pl.BlockSpec((1, wd), lambda bi, ci: (0, 0))

    return pl.pallas_call(
        functools.partial(_hgrn_prompt_kernel, c=c, blk=blk),
        grid=(b, length // c),
        in_specs=[col(0), col(1), col(2), col(3), par(D_MODEL), par(D_MODEL), par(D_MODEL), par(HG_D)],
        out_specs=[pl.BlockSpec((None, c, D_MODEL), lambda bi, ci: (bi, ci, 0)),
                   pl.BlockSpec((None, HG_HEADS, HG_D, HG_D), lambda bi, ci: (bi, 0, 0, 0))],
        out_shape=[jax.ShapeDtypeStruct((b, length, D_MODEL), F32),
                   jax.ShapeDtypeStruct((b, HG_HEADS, HG_D, HG_D), F32)],
        scratch_shapes=[pltpu.VMEM((HG_HEADS, HG_D, HG_D), F32)],
        compiler_params=_params(("parallel", "arbitrary")),
        name="hgrn_prompt",
    )(proj, proj, proj, proj, loglb, log1mlb, omlb, hnorm)


def _hgrn_sample_kernel(q_ref, f_ref, v_ref, gz_ref, s_in_ref, loglb_ref, log1mlb_ref, omlb_ref, hnorm_ref,
                        o_ref, s_out_ref, *, g, n_new):
    def s_get(h, e):
        return s_in_ref[e, h]

    def s_set(h, e, val):
        s_out_ref[e, h] = val

    o_ref[...] = _hgrn_chunk(q_ref[...], f_ref[...], v_ref[...], gz_ref[...], loglb_ref[...], log1mlb_ref[...],
                             omlb_ref[...], hnorm_ref[...], s_get, s_set, g * n_new, n_new)


def _hgrn_sample(proj, s_in, layer, loglb, log1mlb, omlb, hnorm, n_new, g=8):
    b = s_in.shape[1]
    c = g * n_new

    def col(j):
        return pl.BlockSpec((c, D_MODEL), lambda i, j=j: (i, j))

    def par(wd):
        return pl.BlockSpec((1, wd), lambda i: (0, 0))

    st = pl.BlockSpec((g, HG_HEADS, HG_D, HG_D), lambda i: (i, 0, 0, 0))
    st_in = pl.BlockSpec((None, g, HG_HEADS, HG_D, HG_D), lambda i: (layer, i, 0, 0, 0))
    return pl.pallas_call(
        functools.partial(_hgrn_sample_kernel, g=g, n_new=n_new),
        grid=(b // g,),
        in_specs=[col(0), col(1), col(2), col(3), st_in, par(D_MODEL), par(D_MODEL), par(D_MODEL), par(HG_D)],
        out_specs=[pl.BlockSpec((c, D_MODEL), lambda i: (i, 0)), st],
        out_shape=[jax.ShapeDtypeStruct((b * n_new, D_MODEL), F32), jax.ShapeDtypeStruct(s_in.shape[1:], F32)],
        compiler_params=_params(("parallel",)),
        name="hgrn_sample",
    )(proj, proj, proj, proj, s_in, loglb, log1mlb, omlb, hnorm)


def _proj_odd_kernel(x_ref, g_ref, w_ref, o_ref):
    xn = _rms(x_ref[...], g_ref[...]).astype(BF16)
    for j in range(4):
        o_ref[:, j * D_MODEL:(j + 1) * D_MODEL] = jnp.dot(xn, w_ref[:, j * D_MODEL:(j + 1) * D_MODEL],
                                                          preferred_element_type=F32)


def _proj_odd(x, g, w, tm):
    t = x.shape[0]
    n = w.shape[1]
    return pl.pallas_call(
        _proj_odd_kernel,
        grid=(t // tm,),
        in_specs=[pl.BlockSpec((tm, D_MODEL), lambda i: (i, 0)), pl.BlockSpec((1, D_MODEL), lambda i: (0, 0)),
                  pl.BlockSpec((D_MODEL, n), lambda i: (0, 0))],
        out_specs=pl.BlockSpec((tm, n), lambda i: (i, 0)),
        out_shape=jax.ShapeDtypeStruct((t, n), F32),
        compiler_params=_params(("parallel",)),
        name="proj_odd",
    )(x, g, w)


def _post_kernel(*refs, n_mix, final):
    x_ref = refs[0]
    mix_refs = refs[1:1 + n_mix]
    wout_ref, g_ref, wup_ref, wdn_ref = refs[1 + n_mix:5 + n_mix]
    gfin_ref = refs[5 + n_mix] if final else None
    o_ref = refs[-1]
    mixed = None
    row = 0
    for m_ref in mix_refs:
        wd = m_ref.shape[1]
        part = jnp.dot(m_ref[...].astype(BF16), wout_ref[row:row + wd, :], preferred_element_type=F32)
        mixed = part if mixed is None else mixed + part
        row += wd
    h = x_ref[...] + mixed
    hn = _rms(h, g_ref[...]).astype(BF16)
    fc = 1024
    mlp = None
    for j in range(D_FF // fc):
        up = jnp.dot(hn, wup_ref[:, j * fc:(j + 1) * fc], preferred_element_type=F32)
        act = jnp.square(jnp.maximum(up, 0.0)).astype(BF16)
        part = jnp.dot(act, wdn_ref[j * fc:(j + 1) * fc, :], preferred_element_type=F32)
        mlp = part if mlp is None else mlp + part
    h = h + mlp
    o_ref[...] = _rms(h, gfin_ref[...]) if final else h


def _post(x, mixes, wout, g, wup, wdn, gfin, tm):
    t = x.shape[0]
    final = gfin is not None

    def const(a):
        return pl.BlockSpec(a.shape, lambda i: (0, 0), pipeline_mode=pl.Buffered(1))

    in_specs = [pl.BlockSpec((tm, D_MODEL), lambda i: (i, 0))]
    in_specs += [pl.BlockSpec((tm, m.shape[1]), lambda i: (i, 0)) for m in mixes]
    args = [x, *mixes, wout, g, wup, wdn]
    in_specs += [const(wout), const(g), const(wup), const(wdn)]
    if final:
        in_specs.append(const(gfin))
        args.append(gfin)
    return pl.pallas_call(
        functools.partial(_post_kernel, n_mix=len(mixes), final=final),
        grid=(t // tm,),
        in_specs=in_specs,
        out_specs=pl.BlockSpec((tm, D_MODEL), lambda i: (i, 0)),
        out_shape=jax.ShapeDtypeStruct((t, D_MODEL), F32),
        compiler_params=_params(("parallel",)),
        name="post",
    )(*args)


def _pack_even_weight(w):
    aq, ak, av, iq, ik, iw, dqkv, db, da, dz = jnp.split(
        w, [512, 640, 768, 1024, 1088, 1092, 2628, 2632, 2636], axis=1)
    z = lambda n: jnp.zeros((w.shape[0], n), w.dtype)
    packed = jnp.concatenate([aq, ak, iq, ik, iw, z(60), av, dqkv, dz, db, da, z(120)], axis=1)
    return packed.astype(BF16)


def _lane_row(vals, offset):
    return jnp.zeros((1, LANES), F32).at[0, offset:offset + vals.shape[0]].set(vals.astype(F32))


def kernel(x_prompt, x_sample, cache_k, cache_v, cache_ik, state_conv, state_delta, state_hgrn, page_table, meta,
           ln_mix, ln_ffn, ln_final, w_in_e, w_out_e, dn_conv, dn_a_log, dn_dt_bias, dn_norm, w_in_o, w_out_o, hg_lb,
           hg_norm, w_up, w_down):
    bp, seq, _ = x_prompt.shape
    bs, n_new, _ = x_sample.shape
    depth = ln_mix.shape[0]
    n_pool, page = cache_k.shape[1], cache_k.shape[2]
    n_past = page_table.shape[1] * page
    lp = PAD_FRONT + N_META + seq
    tp, ts = bp * lp, bs * n_new
    tm_p = 384
    tm_s = min(256, ts)

    hp = jnp.concatenate([jnp.zeros((bp, PAD_FRONT, D_MODEL), F32),
                          jnp.broadcast_to(meta.astype(F32)[None], (bp, N_META, D_MODEL)), x_prompt], axis=1)
    hp = hp.reshape(tp, D_MODEL)
    hs = x_sample.reshape(ts, D_MODEL)

    tab_p = _rope_tables(jnp.arange(lp, dtype=F32) - PAD_FRONT)
    tab_s = _rope_tables(n_past + (jnp.arange(tm_s) % n_new).astype(F32))

    lb_soft = jax.nn.softmax(hg_lb.astype(F32), axis=0)
    lb_all = jnp.cumsum(lb_soft, axis=0) - lb_soft[0]

    pool_kt = jnp.transpose(cache_k, (0, 1, 3, 4, 2))
    pool_vt = jnp.transpose(cache_v, (0, 1, 3, 4, 2))
    pool_ikt = jnp.transpose(cache_ik, (0, 1, 3, 2))

    kp, vp, ikp, cp, dp, hgp = [], [], [], [], [], []
    ks, vs, iks, cs, ds, hgs = [], [], [], [], [], []
    for l in range(depth):
        last = l == depth - 1
        g_mix = ln_mix[l][None, :]
        if l % 2 == 0:
            i = l // 2
            w_in = _pack_even_weight(w_in_e[i])
            w_out = w_out_e[i].astype(BF16)
            cw = dn_conv[i]
            alog = _lane_row(dn_a_log[i], GATE_A)
            dtb = _lane_row(dn_dt_bias[i], GATE_A)
            dnorm = dn_norm[i][None, :]
            q, k, iq4, ikw, ik4, v, dqkv, dz, gate, kb, vo = _proj_even(hp, g_mix, w_in, tab_p, tm_p)
            r3 = lambda a: a.reshape(bp, lp, a.shape[1])
            a_out = _dsa_prompt(r3(q), r3(iq4), r3(ikw), r3(ik4), r3(kb), r3(vo), min(TOPK_MAX, seq // 4))
            d_out, d_state = _delta_prompt(r3(dqkv), r3(gate), r3(dz), cw, alog, dtb, dnorm)
            mix_p = [a_out.reshape(tp, 512), d_out.reshape(tp, 512)]
            kp.append(r3(k)[:, PAD_FRONT:].reshape(bp, lp - PAD_FRONT, A_KV_HEADS, A_HEAD_DIM))
            vp.append(r3(v)[:, PAD_FRONT:].reshape(bp, lp - PAD_FRONT, A_KV_HEADS, A_HEAD_DIM))
            ikp.append(r3(ikw)[:, PAD_FRONT:, 0:IDX_DIM])
            cp.append(r3(dqkv)[:, lp - (CONV_W - 1):])
            dp.append(d_state)
            q, k, iq4, ikw, ik4, v, dqkv, dz, gate, _, _ = _proj_even(hs, g_mix, w_in, tab_s, tm_s)
            r3 = lambda a: a.reshape(bs, n_new, a.shape[1])
            a_out = _dsa_sample(r3(q), r3(iq4), r3(ikw), r3(ik4), r3(k), r3(v), pool_kt, pool_vt, pool_ikt, i,
                                page_table, min(TOPK_MAX, (n_past + n_new) // 4))
            d_out, d_state = _delta_sample(dqkv, gate, dz, state_conv, state_delta, i, cw, alog, dtb, dnorm, n_new)
            mix_s = [a_out.reshape(ts, 512), d_out]
            ks.append(r3(k).reshape(bs, n_new, A_KV_HEADS, A_HEAD_DIM))
            vs.append(r3(v).reshape(bs, n_new, A_KV_HEADS, A_HEAD_DIM))
            iks.append(r3(ikw)[:, :, 0:IDX_DIM])
            full = jnp.concatenate([state_conv[i], r3(dqkv)], axis=1)
            cs.append(full[:, full.shape[1] - (CONV_W - 1):])
            ds.append(d_state)
        else:
            j = l // 2
            w_in = w_in_o[j].astype(BF16)
            w_out = w_out_o[j].astype(BF16)
            lb = lb_all[l][None, :]
            loglb, log1mlb, omlb = jnp.log(lb), jnp.log1p(-lb), 1.0 - lb
            hnorm = hg_norm[j][None, :]
            proj = _proj_odd(hp, g_mix, w_in, tm_p)
            o, st = _hgrn_prompt(proj.reshape(bp, lp, 4 * D_MODEL), loglb, log1mlb, omlb, hnorm)
            mix_p = [o.reshape(tp, D_MODEL)]
            hgp.append(st)
            proj = _proj_odd(hs, g_mix, w_in, tm_s)
            o, st = _hgrn_sample(proj, state_hgrn, j, loglb, log1mlb, omlb, hnorm, n_new)
            mix_s = [o]
            hgs.append(st)
        g_ffn = ln_ffn[l][None, :]
        wu = w_up[l].astype(BF16)
        wd = w_down[l].astype(BF16)
        gfin = ln_final[None, :] if last else None
        hp = _post(hp, mix_p, w_out, g_ffn, wu, wd, gfin, tm_p)
        hs = _post(hs, mix_s, w_out, g_ffn, wu, wd, gfin, tm_s)
    y_prompt = hp.reshape(bp, lp, D_MODEL)[:, PAD_FRONT + N_META:]
    y_sample = hs.reshape(bs, n_new, D_MODEL)
    return (y_prompt, y_sample, jnp.stack(kp), jnp.stack(vp), jnp.stack(ikp), jnp.stack(cp), jnp.stack(dp),
            jnp.stack(hgp), jnp.stack(ks), jnp.stack(vs), jnp.stack(iks), jnp.stack(cs), jnp.stack(ds),
            jnp.stack(hgs))
```
